```python
import math
import jax, jax.numpy as jnp
from jax import lax
import numpy as np

D_MODEL = 2048
BATCH = 1
SEQ = 8192
DEPTH = 2
DEC_BATCH = 4
DEC_SEQ = 2048
PAST_LEN = 128

N_MIXERS = 2
N_LRU_LAYERS = (DEPTH + 1) // 2
N_ATTN_LAYERS = DEPTH // 2
D_RNN = D_MODEL
LRU_BLOCKS = 16
LRU_BW = D_RNN // LRU_BLOCKS
CONV_W = 4
LRU_C = 8.0
HEAD_DIM = 128
N_HEADS = D_MODEL // HEAD_DIM
N_KV = 4
GROUP = N_HEADS // N_KV
WINDOW = 128
BLOCK = 128
QKV_DIM = (N_HEADS + 2 * N_KV) * HEAD_DIM
NUM_BUCKETS = 32
MAX_DISTANCE = 128
D_FF = int(math.ceil(8 * D_MODEL / 3 / 256) * 256)
ALPHA = (2 * DEPTH) ** 0.25
BETA = (8 * DEPTH) ** -0.25
LN_EPS = 1e-5
NEG = -1e30

kernel_name = 'hybrid_rglru_swa_encoder'


def layer_norm(x, g, b):
    xf = x.astype(jnp.float32)
    mu = xf.mean(-1, keepdims=True)
    var = jnp.mean(jnp.square(xf - mu), -1, keepdims=True)
    y = (xf - mu) * lax.rsqrt(var + LN_EPS)
    return (y * g.astype(jnp.float32) + b.astype(jnp.float32)).astype(x.dtype)


def swiglu(x, w_gu, w_down):
    gu = x @ w_gu
    g, u = gu[..., :D_FF], gu[..., D_FF:]
    return (jax.nn.silu(g) * u) @ w_down


def centred_depthwise_conv(x, w, b):
    S = x.shape[1]
    left = CONV_W // 2
    xp = jnp.pad(x, ((0, 0), (left, CONV_W - 1 - left), (0, 0)))
    y = b
    for k in range(CONV_W):
        y = y + xp[:, k:k + S] * w[k]
    return y


def _lin_combine(c1, c2):
    a1, b1 = c1
    a2, b2 = c2
    return a1 * a2, a2 * b1 + b2


def linear_recurrence(a, b, reverse):
    return lax.associative_scan(_lin_combine, (a, b), reverse=reverse, axis=1)[1]


def rglru_block(x, w_in, conv_w, conv_b, gate_w, gate_b, lam, w_out):
    B, S, _ = x.shape
    proj = x @ w_in
    y_branch = jax.nn.gelu(proj[..., :D_RNN])
    xr = centred_depthwise_conv(proj[..., D_RNN:], conv_w, conv_b)
    xb = xr.reshape(B, S, LRU_BLOCKS, LRU_BW)
    g = jnp.einsum('bsni,egnio->egbsno', xb, gate_w) + gate_b[:, :, None, None]
    g = jax.nn.sigmoid(g.astype(jnp.float32)).reshape(2, 2, B, S, D_RNN)
    r, i = g[:, 0], g[:, 1]
    log_a = -LRU_C * r * jax.nn.softplus(-lam.astype(jnp.float32))[:, None, None, :]
    a = jnp.exp(log_a)
    u = jnp.sqrt(-jnp.expm1(2.0 * log_a)) * i * xr.astype(jnp.float32)[None]
    h = linear_recurrence(a[0], u[0], False) + linear_recurrence(a[1], u[1], True)
    return (y_branch * h.astype(x.dtype)) @ w_out


def t5_bucket(rel):
    nb = NUM_BUCKETS // 2
    ret = (rel > 0).astype(np.int32) * nb
    n = np.abs(rel)
    max_exact = nb // 2
    nn = np.maximum(n, 1).astype(np.float32)
    large = max_exact + (np.log(nn / max_exact) / math.log(MAX_DISTANCE / max_exact) * (nb - max_exact)).astype(np.int32)
    large = np.minimum(large, nb - 1)
    return (ret + np.where(n < max_exact, n, large)).astype(np.int32)


def band_structure(S):
    nblk = S // BLOCK
    q = np.arange(BLOCK)[:, None]
    c = np.arange(3 * BLOCK)[None, :]
    rel = c - BLOCK - q
    band = np.abs(rel) <= WINDOW
    key_abs = (np.arange(nblk)[:, None] - 1) * BLOCK + np.arange(3 * BLOCK)[None, :]
    key_ok = (key_abs >= 0) & (key_abs < S)
    mask = band[None] & key_ok[:, None, :]
    return mask, t5_bucket(rel)


def _key_windows(t, B, nblk):
    tp = jnp.pad(t, ((0, 0), (BLOCK, BLOCK), (0, 0), (0, 0)))
    tb = tp.reshape(B, nblk + 2, BLOCK, N_KV, HEAD_DIM)
    return jnp.concatenate([tb[:, :-2], tb[:, 1:-1], tb[:, 2:]], axis=2)


def windowed_gqa(x, w_qkv, sink, w_out, rel_bias):
    B, S, _ = x.shape
    nblk = S // BLOCK
    mask, bucket = band_structure(S)
    qkv = x @ w_qkv
    q = qkv[..., :N_HEADS * HEAD_DIM].reshape(B, nblk, BLOCK, N_KV, GROUP, HEAD_DIM) * (HEAD_DIM ** -0.5)
    k = qkv[..., N_HEADS * HEAD_DIM:(N_HEADS + N_KV) * HEAD_DIM].reshape(B, S, N_KV, HEAD_DIM)
    v = qkv[..., (N_HEADS + N_KV) * HEAD_DIM:].reshape(B, S, N_KV, HEAD_DIM)
    kw = _key_windows(k, B, nblk)
    vw = _key_windows(v, B, nblk)
    bias = jnp.transpose(rel_bias[bucket], (2, 0, 1)).reshape(N_KV, GROUP, BLOCK, 3 * BLOCK)
    logits = jnp.einsum('bnqkgd,bnckd->bnkgqc', q, kw).astype(jnp.float32) + bias.astype(jnp.float32)
    logits = jnp.where(mask[None, :, None, None], logits, NEG)
    s = sink.reshape(N_KV, GROUP).astype(jnp.float32)[None, None, :, :, None, None]
    m = jnp.maximum(logits.max(-1, keepdims=True), s)
    p = jnp.exp(logits - m)
    p = p / (p.sum(-1, keepdims=True) + jnp.exp(s - m))
    o = jnp.einsum('bnkgqc,bnckd->bnqkgd', p.astype(vw.dtype), vw).reshape(B, S, N_HEADS * HEAD_DIM)
    return o @ w_out


def trunk(x, lru_w_in, lru_conv_w, lru_conv_b, lru_gate_w, lru_gate_b, lru_lambda, lru_w_out,
          attn_w_qkv, attn_sink, attn_w_out, rel_bias, ffn_w_gu, ffn_w_down, ln_g, ln_b):
    for l in range(DEPTH):
        j = l // N_MIXERS
        if l % N_MIXERS == 0:
            h = rglru_block(x, lru_w_in[j], lru_conv_w[j], lru_conv_b[j], lru_gate_w[j],
                            lru_gate_b[j], lru_lambda[j], lru_w_out[j])
        else:
            h = windowed_gqa(x, attn_w_qkv[j], attn_sink[j], attn_w_out[j], rel_bias)
        x = layer_norm(ALPHA * x + h, ln_g[l, 0], ln_b[l, 0])
        x = layer_norm(ALPHA * x + swiglu(x, ffn_w_gu[l], ffn_w_down[l]), ln_g[l, 1], ln_b[l, 1])
    return x


def setup_inputs(seed: int = 0) -> dict:
    key = jax.random.key(seed)
    ks = jax.random.split(key, 20)
    nrm = jax.random.normal
    f32 = jnp.float32
    u = jax.random.uniform(ks[7], (N_LRU_LAYERS, 2, D_RNN), f32, 0.9, 0.999)
    s = u ** (1.0 / LRU_C)
    return {
        'x_prompt': nrm(ks[0], (BATCH, SEQ, D_MODEL), f32),
        'x_sample': nrm(ks[1], (DEC_BATCH, DEC_SEQ, D_MODEL), f32),
        'lru_w_in': nrm(ks[2], (N_LRU_LAYERS, D_MODEL, 2 * D_RNN), f32) * D_MODEL ** -0.5,
        'lru_conv_w': nrm(ks[3], (N_LRU_LAYERS, CONV_W, D_RNN), f32) * CONV_W ** -0.5,
        'lru_conv_b': nrm(ks[4], (N_LRU_LAYERS, D_RNN), f32) * 0.01,
        'lru_gate_w': nrm(ks[5], (N_LRU_LAYERS, 2, 2, LRU_BLOCKS, LRU_BW, LRU_BW), f32) * LRU_BW ** -0.5,
        'lru_gate_b': nrm(ks[6], (N_LRU_LAYERS, 2, 2, LRU_BLOCKS, LRU_BW), f32) * 0.01,
        'lru_lambda': jnp.log(s) - jnp.log1p(-s),
        'lru_w_out': nrm(ks[8], (N_LRU_LAYERS, D_RNN, D_MODEL), f32) * (D_RNN ** -0.5 * BETA),
        'attn_w_qkv': nrm(ks[9], (N_ATTN_LAYERS, D_MODEL, QKV_DIM), f32) * D_MODEL ** -0.5,
        'attn_sink': nrm(ks[10], (N_ATTN_LAYERS, N_HEADS), f32) * 0.5,
        'attn_w_out': nrm(ks[11], (N_ATTN_LAYERS, N_HEADS * HEAD_DIM, D_MODEL), f32) * ((N_HEADS * HEAD_DIM) ** -0.5 * BETA),
        'rel_bias': nrm(ks[12], (NUM_BUCKETS, N_HEADS), f32) * 0.2,
        'ffn_w_gu': nrm(ks[13], (DEPTH, D_MODEL, 2 * D_FF), f32) * D_MODEL ** -0.5,
        'ffn_w_down': nrm(ks[14], (DEPTH, D_FF, D_MODEL), f32) * (D_FF ** -0.5 * BETA),
        'ln_g': 1.0 + 0.02 * nrm(ks[15], (DEPTH, 2, D_MODEL), f32),
        'ln_b': 0.02 * nrm(ks[16], (DEPTH, 2, D_MODEL), f32),
    }


def reference(x_prompt, x_sample, lru_w_in, lru_conv_w, lru_conv_b, lru_gate_w, lru_gate_b, lru_lambda,
              lru_w_out, attn_w_qkv, attn_sink, attn_w_out, rel_bias, ffn_w_gu, ffn_w_down, ln_g, ln_b):
    y_prompt = trunk(x_prompt, lru_w_in, lru_conv_w, lru_conv_b, lru_gate_w, lru_gate_b, lru_lambda, lru_w_out,
                     attn_w_qkv, attn_sink, attn_w_out, rel_bias, ffn_w_gu, ffn_w_down, ln_g, ln_b)
    y_sample = trunk(x_sample, lru_w_in, lru_conv_w, lru_conv_b, lru_gate_w, lru_gate_b, lru_lambda, lru_w_out,
                     attn_w_qkv, attn_sink, attn_w_out, rel_bias, ffn_w_gu, ffn_w_down, ln_g, ln_b)
    return (y_prompt, y_sample)
```

```python
import functools
import math

import jax
import jax.numpy as jnp
import numpy as np
from jax import lax
from jax.experimental import pallas as pl
from jax.experimental.pallas import tpu as pltpu

D_MODEL = 2048
BATCH, SEQ = 1, 8192
DEC_BATCH, DEC_SEQ = 4, 2048
PROMPT_ROWS = BATCH * SEQ
T_ROWS = PROMPT_ROWS + DEC_BATCH * DEC_SEQ
DEPTH = 2

D_RNN = D_MODEL
LRU_BLOCKS = 16
LRU_BW = D_RNN // LRU_BLOCKS
CONV_W = 4
CONV_LEFT = CONV_W // 2
LRU_C = 8.0

HEAD_DIM = 128
N_HEADS = D_MODEL // HEAD_DIM
N_KV = 4
GROUP = N_HEADS // N_KV
WINDOW = 128
BLOCK = 128
KV_DIM = N_KV * HEAD_DIM
NUM_BUCKETS = 32
MAX_DISTANCE = 128

D_FF = int(math.ceil(8 * D_MODEL / 3 / 256) * 256)
ALPHA = (2 * DEPTH) ** 0.25
LN_EPS = 1e-5
NEG = -1e30

F32 = jnp.float32
BF16 = jnp.bfloat16

SUBLANES_F32 = 8
SUBLANES_BF16 = 16
VMEM_LIMIT_BYTES = 56 * 1024 * 1024

LRU_TM = 256
FFN_TM = 512
FFN_TF = 512
QKV_TM = 512
ATT_TQ = 512


def _seq_start(r0):
    return jnp.where(r0 < PROMPT_ROWS, r0 % SEQ == 0, (r0 - PROMPT_ROWS) % DEC_SEQ == 0)


def _seq_end(r1):
    return jnp.where(r1 <= PROMPT_ROWS, r1 % SEQ == 0, (r1 - PROMPT_ROWS) % DEC_SEQ == 0)


def _layer_norm(v, g, b):
    mu = jnp.mean(v, axis=-1, keepdims=True)
    d = v - mu
    var = jnp.mean(d * d, axis=-1, keepdims=True)
    return d * lax.rsqrt(var + LN_EPS) * g + b


def _dot(a, b):
    return jnp.dot(a, b, preferred_element_type=F32)


def _gate_coeffs(xr_blk, gw, gb_r, gb_i, c_blk):
    g = _dot(xr_blk.astype(BF16), gw)
    r = jax.nn.sigmoid(g[:, :LRU_BW] + gb_r)
    ig = jax.nn.sigmoid(g[:, LRU_BW:] + gb_i)
    log_a = c_blk * r
    th = jnp.tanh(log_a)
    a = jnp.exp(log_a)
    u = jnp.sqrt(-2.0 * th / (1.0 - th)) * ig * xr_blk
    return a, u


def _decay_scale(lam_row):
    z = -lam_row
    return -LRU_C * (jnp.maximum(z, 0.0) + jnp.log1p(jnp.exp(-jnp.abs(z))))


def _lru_fwd_kernel(x_ref, xn_ref, win_ref, cw_ref, cb_ref, gw_ref, gb_ref, lam_ref,
                    y_ref, xr_ref, hf_ref,
                    lhs_sc, p_sc, a_sc, u_sc, h_sc):
    tm = LRU_TM
    i = pl.program_id(0)
    r0 = i * tm
    start = _seq_start(r0)
    end = _seq_end(r0 + tm)

    lhs_sc[0:tm, :] = x_ref[...].astype(BF16)
    lhs_sc[tm:tm + SUBLANES_BF16, :] = xn_ref[...].astype(BF16)

    @pl.when(end)
    def _():
        lhs_sc[tm:tm + SUBLANES_BF16, :] = jnp.zeros((SUBLANES_BF16, D_MODEL), BF16)

    y_ref[...] = jax.nn.gelu(_dot(lhs_sc[0:tm, :], win_ref[:, 0:D_RNN]))

    @pl.when(start)
    def _():
        p_sc[0:SUBLANES_F32, :] = jnp.zeros((SUBLANES_F32, D_RNN), F32)
        h_sc[...] = jnp.zeros_like(h_sc)

    p_sc[SUBLANES_F32:, :] = _dot(lhs_sc[...], win_ref[:, D_RNN:])

    c = _decay_scale(lam_ref[...])
    for n in range(LRU_BLOCKS):
        sl = slice(n * LRU_BW, (n + 1) * LRU_BW)
        xr_blk = cb_ref[:, sl]
        for k in range(CONV_W):
            xr_blk = xr_blk + cw_ref[k:k + 1, sl] * p_sc[pl.ds(SUBLANES_F32 - CONV_LEFT + k, tm), sl]
        xr_ref[:, sl] = xr_blk
        a, u = _gate_coeffs(xr_blk, gw_ref[n], gb_ref[0:1, sl], gb_ref[1:2, sl], c[:, sl])
        a_sc[:, sl] = a
        u_sc[:, sl] = u

    p_sc[0:SUBLANES_F32, :] = p_sc[tm:tm + SUBLANES_F32, :]

    def step(t, h):
        h = a_sc[pl.ds(t, 1), :] * h + u_sc[pl.ds(t, 1), :]
        hf_ref[pl.ds(t, 1), :] = h
        return h

    h_sc[0:1, :] = lax.fori_loop(0, tm, step, h_sc[0:1, :], unroll=8)


def _lru_fwd(x, w_in, conv_w, conv_b, gw, gb, lam):
    tm = LRU_TM
    nt = T_ROWS // tm
    nb16 = T_ROWS // SUBLANES_BF16
    const = lambda i: (0, 0)
    row = pl.BlockSpec((tm, D_MODEL), lambda i: (i, 0))
    return pl.pallas_call(
        _lru_fwd_kernel,
        out_shape=[jax.ShapeDtypeStruct((T_ROWS, D_RNN), F32)] * 3,
        grid=(nt,),
        in_specs=[
            row,
            pl.BlockSpec((SUBLANES_BF16, D_MODEL),
                         lambda i: (jnp.minimum((i + 1) * (tm // SUBLANES_BF16), nb16 - 1), 0)),
            pl.BlockSpec((D_MODEL, 2 * D_RNN), const, pipeline_mode=pl.Buffered(1)),
            pl.BlockSpec((CONV_W, D_RNN), const),
            pl.BlockSpec((1, D_RNN), const),
            pl.BlockSpec((LRU_BLOCKS, LRU_BW, 2 * LRU_BW), lambda i: (0, 0, 0)),
            pl.BlockSpec((2, D_RNN), const),
            pl.BlockSpec((1, D_RNN), const),
        ],
        out_specs=[row, row, row],
        scratch_shapes=[
            pltpu.VMEM((tm + SUBLANES_BF16, D_MODEL), BF16),
            pltpu.VMEM((SUBLANES_F32 + tm + SUBLANES_BF16, D_RNN), F32),
            pltpu.VMEM((tm, D_RNN), F32),
            pltpu.VMEM((tm, D_RNN), F32),
            pltpu.VMEM((SUBLANES_F32, D_RNN), F32),
        ],
        compiler_params=pltpu.CompilerParams(dimension_semantics=("arbitrary",),
                                             vmem_limit_bytes=VMEM_LIMIT_BYTES),
        name="lru_fwd",
    )(x, x, w_in, conv_w, conv_b, gw, gb, lam)


def _lru_bwd_kernel(xr_ref, y_ref, hf_ref, x_ref, gw_ref, gb_ref, lam_ref, wout_ref, g_ref, b_ref,
                    o_ref, a_sc, u_sc, hb_sc, h_sc):
    tm = LRU_TM
    nt = pl.num_programs(0)
    r0 = (nt - 1 - pl.program_id(0)) * tm

    @pl.when(_seq_end(r0 + tm))
    def _():
        h_sc[...] = jnp.zeros_like(h_sc)

    c = _decay_scale(lam_ref[...])
    for n in range(LRU_BLOCKS):
        sl = slice(n * LRU_BW, (n + 1) * LRU_BW)
        a, u = _gate_coeffs(xr_ref[:, sl], gw_ref[n], gb_ref[0:1, sl], gb_ref[1:2, sl], c[:, sl])
        a_sc[:, sl] = a
        u_sc[:, sl] = u

    def step(s, h):
        t = tm - 1 - s
        h = a_sc[pl.ds(t, 1), :] * h + u_sc[pl.ds(t, 1), :]
        hb_sc[pl.ds(t, 1), :] = h
        return h

    h_sc[0:1, :] = lax.fori_loop(0, tm, step, h_sc[0:1, :], unroll=8)

    mixed = (y_ref[...] * (hf_ref[...] + hb_sc[...])).astype(BF16)
    z = _dot(mixed, wout_ref[...])
    o_ref[...] = _layer_norm(ALPHA * x_ref[...] + z, g_ref[...], b_ref[...])


def _lru_bwd(xr, y, hf, x, gw, gb, lam, w_out, ln_g, ln_b):
    tm = LRU_TM
    nt = T_ROWS // tm
    const = lambda i: (0, 0)
    row = pl.BlockSpec((tm, D_MODEL), lambda i: (nt - 1 - i, 0))
    return pl.pallas_call(
        _lru_bwd_kernel,
        out_shape=jax.ShapeDtypeStruct((T_ROWS, D_MODEL), F32),
        grid=(nt,),
        in_specs=[
            row, row, row, row,
            pl.BlockSpec((LRU_BLOCKS, LRU_BW, 2 * LRU_BW), lambda i: (0, 0, 0)),
            pl.BlockSpec((2, D_RNN), const),
            pl.BlockSpec((1, D_RNN), const),
            pl.BlockSpec((D_RNN, D_MODEL), const, pipeline_mode=pl.Buffered(1)),
            pl.BlockSpec((1, D_MODEL), const),
            pl.BlockSpec((1, D_MODEL), const),
        ],
        out_specs=row,
        scratch_shapes=[
            pltpu.VMEM((tm, D_RNN), F32),
            pltpu.VMEM((tm, D_RNN), F32),
            pltpu.VMEM((tm, D_RNN), F32),
            pltpu.VMEM((SUBLANES_F32, D_RNN), F32),
        ],
        compiler_params=pltpu.CompilerParams(dimension_semantics=("arbitrary",),
                                             vmem_limit_bytes=VMEM_LIMIT_BYTES),
        name="lru_bwd",
    )(xr, y, hf, x, gw, gb, lam, w_out, ln_g, ln_b)


def _ffn_kernel(x_ref, wg_ref, wu_ref, wd_ref, g_ref, b_ref, o_ref, xb_sc, acc_sc):
    j = pl.program_id(1)

    @pl.when(j == 0)
    def _():
        xb_sc[...] = x_ref[...].astype(BF16)
        acc_sc[...] = jnp.zeros_like(acc_sc)

    xb = xb_sc[...]
    act = jax.nn.silu(_dot(xb, wg_ref[...])) * _dot(xb, wu_ref[...])
    acc_sc[...] += _dot(act.astype(BF16), wd_ref[...])

    @pl.when(j == pl.num_programs(1) - 1)
    def _():
        o_ref[...] = _layer_norm(ALPHA * x_ref[...] + acc_sc[...], g_ref[...], b_ref[...])


def _ffn(x, w_gu, w_down, ln_g, ln_b):
    tm, tf = FFN_TM, FFN_TF
    nf = D_FF // tf
    row = pl.BlockSpec((tm, D_MODEL), lambda i, j: (i, 0))
    vec = pl.BlockSpec((1, D_MODEL), lambda i, j: (0, 0))
    return pl.pallas_call(
        _ffn_kernel,
        out_shape=jax.ShapeDtypeStruct((T_ROWS, D_MODEL), F32),
        grid=(T_ROWS // tm, nf),
        in_specs=[
            row,
            pl.BlockSpec((D_MODEL, tf), lambda i, j: (0, j)),
            pl.BlockSpec((D_MODEL, tf), lambda i, j: (0, nf + j)),
            pl.BlockSpec((tf, D_MODEL), lambda i, j: (j, 0)),
            vec, vec,
        ],
        out_specs=row,
        scratch_shapes=[pltpu.VMEM((tm, D_MODEL), BF16), pltpu.VMEM((tm, D_MODEL), F32)],
        compiler_params=pltpu.CompilerParams(dimension_semantics=("arbitrary", "arbitrary"),
                                             vmem_limit_bytes=VMEM_LIMIT_BYTES),
        name="ffn",
    )(x, w_gu, w_gu, w_down, ln_g, ln_b)


def _qkv_kernel(x_ref, w_ref, q_ref, k_ref, v_ref):
    qkv = _dot(x_ref[...].astype(BF16), w_ref[...])
    q_ref[...] = (qkv[:, :D_MODEL] * (HEAD_DIM ** -0.5)).astype(BF16)
    k_ref[...] = qkv[:, D_MODEL:D_MODEL + KV_DIM].astype(BF16)
    v_ref[...] = qkv[:, D_MODEL + KV_DIM:].astype(BF16)


def _qkv(x, w_qkv):
    tm = QKV_TM
    return pl.pallas_call(
        _qkv_kernel,
        out_shape=[jax.ShapeDtypeStruct((T_ROWS, D_MODEL), BF16),
                   jax.ShapeDtypeStruct((T_ROWS, KV_DIM), BF16),
                   jax.ShapeDtypeStruct((T_ROWS, KV_DIM), BF16)],
        grid=(T_ROWS // tm,),
        in_specs=[
            pl.BlockSpec((tm, D_MODEL), lambda i: (i, 0)),
            pl.BlockSpec((D_MODEL, D_MODEL + 2 * KV_DIM), lambda i: (0, 0), pipeline_mode=pl.Buffered(1)),
        ],
        out_specs=[pl.BlockSpec((tm, D_MODEL), lambda i: (i, 0)),
                   pl.BlockSpec((tm, KV_DIM), lambda i: (i, 0)),
                   pl.BlockSpec((tm, KV_DIM), lambda i: (i, 0))],
        compiler_params=pltpu.CompilerParams(dimension_semantics=("arbitrary",),
                                             vmem_limit_bytes=VMEM_LIMIT_BYTES),
        name="qkv",
    )(x, w_qkv)


def _t5_bucket(rel):
    nb = NUM_BUCKETS // 2
    ret = (rel > 0).astype(np.int32) * nb
    n = np.abs(rel)
    max_exact = nb // 2
    nn = np.maximum(n, 1).astype(np.float32)
    large = max_exact + (np.log(nn / max_exact) / math.log(MAX_DISTANCE / max_exact)
                         * (nb - max_exact)).astype(np.int32)
    large = np.minimum(large, nb - 1)
    return (ret + np.where(n < max_exact, n, large)).astype(np.int32)


def _bias_kernel(bucket_ref, rb_ref, o_ref):
    bucket = bucket_ref[...]
    for h in range(N_HEADS):
        acc = jnp.zeros((BLOCK, 3 * BLOCK), F32)
        for b in range(NUM_BUCKETS):
            acc = jnp.where(bucket == b, rb_ref[b, h], acc)
        o_ref[h] = acc


def _bias_table(rel_bias):
    q = np.arange(BLOCK)[:, None]
    c = np.arange(3 * BLOCK)[None, :]
    bucket = jnp.asarray(_t5_bucket(c - BLOCK - q))
    return pl.pallas_call(
        _bias_kernel,
        out_shape=jax.ShapeDtypeStruct((N_HEADS, BLOCK, 3 * BLOCK), F32),
        in_specs=[pl.BlockSpec(memory_space=pltpu.VMEM), pl.BlockSpec(memory_space=pltpu.SMEM)],
        out_specs=pl.BlockSpec(memory_space=pltpu.VMEM),
        name="bias",
    )(bucket, rel_bias)


def _attn_kernel(q_ref, k_ref, kp_ref, kn_ref, v_ref, vp_ref, vn_ref, bias_ref, sink_ref,
                 x_ref, wout_ref, g_ref, b_ref, o_ref, kw_sc, vw_sc, ao_sc):
    tq = ATT_TQ
    nblk = tq // BLOCK
    r0 = pl.program_id(0) * tq
    start = _seq_start(r0)
    end = _seq_end(r0 + tq)

    kw_sc[0:BLOCK, :] = kp_ref[...]
    kw_sc[BLOCK:BLOCK + tq, :] = k_ref[...]
    kw_sc[BLOCK + tq:, :] = kn_ref[...]
    vw_sc[0:BLOCK, :] = vp_ref[...]
    vw_sc[BLOCK:BLOCK + tq, :] = v_ref[...]
    vw_sc[BLOCK + tq:, :] = vn_ref[...]

    rows = GROUP * BLOCK
    qpos = lax.broadcasted_iota(jnp.int32, (rows, 3 * BLOCK), 0) % BLOCK
    col = lax.broadcasted_iota(jnp.int32, (rows, 3 * BLOCK), 1)
    band = jnp.abs(col - BLOCK - qpos) <= WINDOW
    head_of_row = lax.broadcasted_iota(jnp.int32, (rows, 1), 0) // BLOCK

    for blk in range(nblk):
        mask = band
        if blk == 0:
            mask = mask & (col >= jnp.where(start, BLOCK, 0))
        if blk == nblk - 1:
            mask = mask & (col < jnp.where(end, 2 * BLOCK, 3 * BLOCK))
        qrows = slice(blk * BLOCK, (blk + 1) * BLOCK)
        for kh in range(N_KV):
            ksl = slice(kh * HEAD_DIM, (kh + 1) * HEAD_DIM)
            qs = jnp.concatenate(
                [q_ref[qrows, (kh * GROUP + g) * HEAD_DIM:(kh * GROUP + g + 1) * HEAD_DIM]
                 for g in range(GROUP)], axis=0)
            kw = kw_sc[blk * BLOCK:blk * BLOCK + 3 * BLOCK, ksl]
            vw = vw_sc[blk * BLOCK:blk * BLOCK + 3 * BLOCK, ksl]
            logits = lax.dot_general(qs, kw, (((1,), (1,)), ((), ())), preferred_element_type=F32)
            logits = logits + bias_ref[kh * GROUP:(kh + 1) * GROUP].reshape(rows, 3 * BLOCK)
            logits = jnp.where(mask, logits, NEG)
            s = jnp.zeros((rows, 1), F32)
            for g in range(GROUP):
                s = jnp.where(head_of_row == g, sink_ref[0, kh * GROUP + g], s)
            m = jnp.maximum(jnp.max(logits, axis=-1, keepdims=True), s)
            p = jnp.exp(logits - m)
            denom = jnp.sum(p, axis=-1, keepdims=True) + jnp.exp(s - m)
            p = p * (1.0 / denom)
            o = _dot(p.astype(BF16), vw)
            for g in range(GROUP):
                h = kh * GROUP + g
                ao_sc[qrows, h * HEAD_DIM:(h + 1) * HEAD_DIM] = o[g * BLOCK:(g + 1) * BLOCK, :].astype(BF16)

    z = _dot(ao_sc[...], wout_ref[...])
    o_ref[...] = _layer_norm(ALPHA * x_ref[...] + z, g_ref[...], b_ref[...])


def _attn(q, k, v, bias, sink, x, w_out, ln_g, ln_b):
    tq = ATT_TQ
    per = tq // BLOCK
    nblocks = T_ROWS // BLOCK
    const = lambda i: (0, 0)
    main_kv = pl.BlockSpec((tq, KV_DIM), lambda i: (i, 0))
    prev_kv = pl.BlockSpec((BLOCK, KV_DIM), lambda i: (jnp.maximum(i * per - 1, 0), 0))
    next_kv = pl.BlockSpec((BLOCK, KV_DIM), lambda i: (jnp.minimum((i + 1) * per, nblocks - 1), 0))
    row = pl.BlockSpec((tq, D_MODEL), lambda i: (i, 0))
    return pl.pallas_call(
        _attn_kernel,
        out_shape=jax.ShapeDtypeStruct((T_ROWS, D_MODEL), F32),
        grid=(T_ROWS // tq,),
        in_specs=[
            row, main_kv, prev_kv, next_kv, main_kv, prev_kv, next_kv,
            pl.BlockSpec((N_HEADS, BLOCK, 3 * BLOCK), lambda i: (0, 0, 0), pipeline_mode=pl.Buffered(1)),
            pl.BlockSpec(memory_space=pltpu.SMEM),
            row,
            pl.BlockSpec((N_HEADS * HEAD_DIM, D_MODEL), const, pipeline_mode=pl.Buffered(1)),
            pl.BlockSpec((1, D_MODEL), const),
            pl.BlockSpec((1, D_MODEL), const),
        ],
        out_specs=row,
        scratch_shapes=[
            pltpu.VMEM((tq + 2 * BLOCK, KV_DIM), BF16),
            pltpu.VMEM((tq + 2 * BLOCK, KV_DIM), BF16),
            pltpu.VMEM((tq, N_HEADS * HEAD_DIM), BF16),
        ],
        compiler_params=pltpu.CompilerParams(dimension_semantics=("arbitrary",),
                                             vmem_limit_bytes=VMEM_LIMIT_BYTES),
        name="attn",
    )(q, k, k, k, v, v, v, bias, sink, x, w_out, ln_g, ln_b)


def _pack_gates(gate_w_dir):
    return jnp.concatenate([gate_w_dir[0], gate_w_dir[1]], axis=-1).astype(BF16)


def kernel(x_prompt, x_sample, lru_w_in, lru_conv_w, lru_conv_b, lru_gate_w, lru_gate_b, lru_lambda,
           lru_w_out, attn_w_qkv, attn_sink, attn_w_out, rel_bias, ffn_w_gu, ffn_w_down, ln_g, ln_b):
    x = jnp.concatenate([x_prompt.reshape(PROMPT_ROWS, D_MODEL),
                         x_sample.reshape(DEC_BATCH * DEC_SEQ, D_MODEL)], axis=0)
    vec = lambda a: a.reshape(1, -1)

    gb = lru_gate_b[0].reshape(2, 2, D_RNN)
    y, xr, hf = _lru_fwd(x, lru_w_in[0].astype(BF16), lru_conv_w[0], vec(lru_conv_b[0]),
                         _pack_gates(lru_gate_w[0, 0]), gb[0], vec(lru_lambda[0, 0]))
    x = _lru_bwd(xr, y, hf, x, _pack_gates(lru_gate_w[0, 1]), gb[1], vec(lru_lambda[0, 1]),
                 lru_w_out[0].astype(BF16), vec(ln_g[0, 0]), vec(ln_b[0, 0]))
    x = _ffn(x, ffn_w_gu[0].astype(BF16), ffn_w_down[0].astype(BF16), vec(ln_g[0, 1]), vec(ln_b[0, 1]))

    q, k, v = _qkv(x, attn_w_qkv[0].astype(BF16))
    x = _attn(q, k, v, _bias_table(rel_bias), vec(attn_sink[0]), x, attn_w_out[0].astype(BF16),
              vec(ln_g[1, 0]), vec(ln_b[1, 0]))
    x = _ffn(x, ffn_w_gu[1].astype(BF16), ffn_w_down[1].astype(BF16), vec(ln_g[1, 1]), vec(ln_b[1, 1]))

    return (x[:PROMPT_ROWS].reshape(BATCH, SEQ, D_MODEL),
            x[PROMPT_ROWS:].reshape(DEC_BATCH, DEC_SEQ, D_MODEL))
```
